```python
import math
import jax, jax.numpy as jnp
from jax import lax
import numpy as np

D_MODEL = 4096
BATCH = 2
SEQ = 8192
DEPTH = 2

CHUNK = 64
N_A = DEPTH // 2
N_B = DEPTH - N_A
D_FF = 11008
CONV_WIDTH = 31
HEAD_DIM = 128
N_HEADS = D_MODEL // HEAD_DIM
Q_BLOCK = 128
NORM_EPS = 1e-6

kernel_name = "yoco_conformer_conv_fox_hybrid"


def rms_norm(x, g):
    xf = x.astype(jnp.float32)
    y = xf * lax.rsqrt(jnp.mean(xf * xf, axis=-1, keepdims=True) + NORM_EPS)
    return (y * g.astype(jnp.float32)).astype(x.dtype)


def layer_norm(x, g, b):
    xf = x.astype(jnp.float32)
    mu = jnp.mean(xf, axis=-1, keepdims=True)
    var = jnp.mean(jnp.square(xf - mu), axis=-1, keepdims=True)
    y = (xf - mu) * lax.rsqrt(var + NORM_EPS)
    return (y * g.astype(jnp.float32) + b.astype(jnp.float32)).astype(x.dtype)


def swiglu(h, w1, w3, w2):
    return (jax.nn.silu(h @ w1) * (h @ w3)) @ w2


def conformer_conv(h, pw1_w, pw1_b, dw_w, dw_b, ln_g, ln_b, pw2_w, pw2_b):
    a, gate = jnp.split(h @ pw1_w + pw1_b, 2, axis=-1)
    u = a * jax.nn.sigmoid(gate)
    u = lax.conv_general_dilated(
        u, dw_w[:, None, :].astype(u.dtype),
        window_strides=(1,), padding=[(CONV_WIDTH - 1, 0)],
        dimension_numbers=("NWC", "WIO", "NWC"),
        feature_group_count=D_MODEL) + dw_b
    u = jax.nn.silu(layer_norm(u, ln_g, ln_b))
    return u @ pw2_w + pw2_b


def shared_kv(y, kv_norm, w_kvf, b_f):
    B, S, _ = y.shape
    proj = rms_norm(y, kv_norm) @ w_kvf
    k = proj[..., :D_MODEL].reshape(B, S, N_HEADS, HEAD_DIM).transpose(0, 2, 1, 3)
    v = proj[..., D_MODEL:2 * D_MODEL].reshape(B, S, N_HEADS, HEAD_DIM).transpose(0, 2, 1, 3)
    f_logit = (proj[..., 2 * D_MODEL:] + b_f).astype(jnp.float32)
    c = jnp.cumsum(jax.nn.log_sigmoid(f_logit), axis=1)
    return k, v, c.transpose(0, 2, 1)


def forgetting_attention(h, w_q, w_o, k, v, c):
    B, S, _ = h.shape
    n_blk = S // Q_BLOCK
    scale = 1.0 / math.sqrt(HEAD_DIM)
    q = (h @ w_q).reshape(B, n_blk, Q_BLOCK, N_HEADS, HEAD_DIM).transpose(1, 0, 3, 2, 4)
    cq = c.reshape(B, N_HEADS, n_blk, Q_BLOCK).transpose(2, 0, 1, 3)
    key_pos = jnp.arange(S)

    def one_block(args):
        qi, cqi, i = args
        s = jnp.einsum("bhqd,bhkd->bhqk", qi, k,
                       preferred_element_type=jnp.float32) * scale
        s = s + cqi[..., None] - c[:, :, None, :]
        q_pos = i * Q_BLOCK + jnp.arange(Q_BLOCK)
        mask = key_pos[None, :] <= q_pos[:, None]
        p = jax.nn.softmax(jnp.where(mask, s, -jnp.inf), axis=-1)
        return jnp.einsum("bhqk,bhkd->bhqd", p.astype(v.dtype), v)

    o = lax.map(one_block, (q, cq, jnp.arange(n_blk)))
    o = o.transpose(1, 0, 3, 2, 4).reshape(B, S, D_MODEL)
    return o @ w_o


def setup_inputs(seed: int = 0) -> dict:
    key = jax.random.key(seed)
    ks = jax.random.split(key, 32)
    D, F, H, W = D_MODEL, D_FF, N_HEADS, CONV_WIDTH
    nrm = lambda k, shape, fan_in: jax.random.normal(k, shape, jnp.float32) * (fan_in ** -0.5)
    gain = lambda k, shape: 1.0 + 0.05 * jax.random.normal(k, shape, jnp.float32)
    small = lambda k, shape: 0.02 * jax.random.normal(k, shape, jnp.float32)
    return {
        "x": jax.random.normal(ks[0], (BATCH, SEQ, D), jnp.float32),
        "norm_ffn1": gain(ks[1], (DEPTH, D)),
        "ffn1_w1": nrm(ks[2], (DEPTH, D, F), D),
        "ffn1_w3": nrm(ks[3], (DEPTH, D, F), D),
        "ffn1_w2": nrm(ks[4], (DEPTH, F, D), F),
        "norm_mix": gain(ks[5], (DEPTH, D)),
        "norm_ffn2": gain(ks[6], (DEPTH, D)),
        "ffn2_w1": nrm(ks[7], (DEPTH, D, F), D),
        "ffn2_w3": nrm(ks[8], (DEPTH, D, F), D),
        "ffn2_w2": nrm(ks[9], (DEPTH, F, D), F),
        "conv_pw1_w": nrm(ks[10], (N_A, D, 2 * D), D),
        "conv_pw1_b": small(ks[11], (N_A, 2 * D)),
        "conv_dw_w": nrm(ks[12], (N_A, W, D), W),
        "conv_dw_b": small(ks[13], (N_A, D)),
        "conv_ln_g": gain(ks[14], (N_A, D)),
        "conv_ln_b": small(ks[15], (N_A, D)),
        "conv_pw2_w": nrm(ks[16], (N_A, D, D), D),
        "conv_pw2_b": small(ks[17], (N_A, D)),
        "kv_norm": gain(ks[18], (D,)),
        "w_kvf": nrm(ks[19], (D, 2 * D + H), D),
        "b_f": 1.0 + 0.5 * jax.random.normal(ks[20], (H,), jnp.float32),
        "attn_wq": nrm(ks[21], (N_B, D, D), D),
        "attn_wo": nrm(ks[22], (N_B, D, D), D),
        "final_norm": gain(ks[23], (D,)),
    }


def reference(x, norm_ffn1, ffn1_w1, ffn1_w3, ffn1_w2, norm_mix, norm_ffn2,
              ffn2_w1, ffn2_w3, ffn2_w2, conv_pw1_w, conv_pw1_b, conv_dw_w,
              conv_dw_b, conv_ln_g, conv_ln_b, conv_pw2_w, conv_pw2_b,
              kv_norm, w_kvf, b_f, attn_wq, attn_wo, final_norm):
    h = x
    kv = None
    for layer in range(DEPTH):
        h = h + 0.5 * swiglu(rms_norm(h, norm_ffn1[layer]),
                             ffn1_w1[layer], ffn1_w3[layer], ffn1_w2[layer])
        hn = rms_norm(h, norm_mix[layer])
        if layer < N_A:
            i = layer
            h = h + conformer_conv(hn, conv_pw1_w[i], conv_pw1_b[i], conv_dw_w[i],
                                   conv_dw_b[i], conv_ln_g[i], conv_ln_b[i],
                                   conv_pw2_w[i], conv_pw2_b[i])
        else:
            j = layer - N_A
            k, v, c = kv
            h = h + forgetting_attention(hn, attn_wq[j], attn_wo[j], k, v, c)
        h = h + 0.5 * swiglu(rms_norm(h, norm_ffn2[layer]),
                             ffn2_w1[layer], ffn2_w3[layer], ffn2_w2[layer])
        if layer == N_A - 1:
            kv = shared_kv(h, kv_norm, w_kvf, b_f)
    return rms_norm(h, final_norm)
```

```python
import functools
import math

import jax
import jax.numpy as jnp
from jax import lax
from jax.experimental import pallas as pl
from jax.experimental.pallas import tpu as pltpu

NORM_EPS = 1e-6
HEAD_DIM = 128
CONV_WIDTH = 31
CONV_HALO = 32
LANES = 128
V7X_VMEM_BYTES = 64 * 1024 * 1024
VMEM_CEILING = V7X_VMEM_BYTES - 8 * 1024 * 1024

F32 = jnp.float32
BF16 = jnp.bfloat16


def _vmem_limit(block_bytes, temp_bytes=0):
    need = 2 * block_bytes + temp_bytes + 4 * 1024 * 1024
    return int(min(max(need, 16 * 1024 * 1024), VMEM_CEILING))


def _nbytes(shape, dtype):
    return math.prod(shape) * jnp.dtype(dtype).itemsize


def _rmsnorm_kernel(x_ref, g_ref, o_ref):
    x = x_ref[...]
    ms = jnp.mean(x * x, axis=-1, keepdims=True)
    o_ref[...] = (x * lax.rsqrt(ms + NORM_EPS) * g_ref[...]).astype(o_ref.dtype)


def rmsnorm(x, g, out_dtype, tm=256):
    T, D = x.shape
    blocks = _nbytes((tm, D), F32) + _nbytes((tm, D), out_dtype)
    return pl.pallas_call(
        _rmsnorm_kernel,
        grid=(T // tm,),
        in_specs=[pl.BlockSpec((tm, D), lambda i: (i, 0)),
                  pl.BlockSpec((1, D), lambda i: (0, 0))],
        out_specs=pl.BlockSpec((tm, D), lambda i: (i, 0)),
        out_shape=jax.ShapeDtypeStruct((T, D), out_dtype),
        compiler_params=pltpu.CompilerParams(
            dimension_semantics=("parallel",),
            vmem_limit_bytes=_vmem_limit(blocks, 2 * _nbytes((tm, D), F32))),
        name="rmsnorm",
    )(x, g.reshape(1, D))


def _gated_kernel(*refs, mode, has_bias):
    if has_bias:
        a_ref, wu_ref, wv_ref, bu_ref, bv_ref, o_ref = refs
    else:
        a_ref, wu_ref, wv_ref, o_ref = refs
    a = a_ref[...]
    u = jnp.dot(a, wu_ref[...], preferred_element_type=F32)
    v = jnp.dot(a, wv_ref[...], preferred_element_type=F32)
    if has_bias:
        u = u + bu_ref[...]
        v = v + bv_ref[...]
    if mode == "swiglu":
        r = u * jax.nn.sigmoid(u) * v
    else:
        r = u * jax.nn.sigmoid(v)
    o_ref[...] = r.astype(o_ref.dtype)


def gated_matmul(a, w, n_out, v_col_offset, bias, mode, out_dtype, tm, tn, w2=None):
    T, K = a.shape
    wv = w if w2 is None else w2
    off = v_col_offset // tn
    has_bias = bias is not None
    in_specs = [pl.BlockSpec((tm, K), lambda i, j: (i, 0)),
                pl.BlockSpec((K, tn), lambda i, j: (0, j)),
                pl.BlockSpec((K, tn), lambda i, j: (0, j + off))]
    args = [a, w, wv]
    if has_bias:
        b2 = bias.reshape(1, -1)
        in_specs += [pl.BlockSpec((1, tn), lambda i, j: (0, j)),
                     pl.BlockSpec((1, tn), lambda i, j: (0, j + off))]
        args += [b2, b2]
    blocks = (_nbytes((tm, K), a.dtype) + 2 * _nbytes((K, tn), w.dtype)
              + _nbytes((tm, tn), out_dtype))
    return pl.pallas_call(
        functools.partial(_gated_kernel, mode=mode, has_bias=has_bias),
        grid=(T // tm, n_out // tn),
        in_specs=in_specs,
        out_specs=pl.BlockSpec((tm, tn), lambda i, j: (i, j)),
        out_shape=jax.ShapeDtypeStruct((T, n_out), out_dtype),
        compiler_params=pltpu.CompilerParams(
            dimension_semantics=("parallel", "parallel"),
            vmem_limit_bytes=_vmem_limit(blocks, 5 * _nbytes((tm, tn), F32))),
        name="gated_" + mode,
    )(*args)


def _matmul_kernel(*refs, scale, has_bias, has_res):
    a_ref, w_ref = refs[0], refs[1]
    o_ref = refs[-1]
    acc = jnp.dot(a_ref[...], w_ref[...], preferred_element_type=F32)
    pos = 2
    if has_bias:
        acc = acc + refs[pos][...]
        pos += 1
    if scale != 1.0:
        acc = acc * scale
    if has_res:
        acc = acc + refs[pos][...]
    o_ref[...] = acc.astype(o_ref.dtype)


def matmul(a, w, out_dtype, tm, tn, bias=None, residual=None, scale=1.0, name="matmul"):
    T, K = a.shape
    N = w.shape[1]
    in_specs = [pl.BlockSpec((tm, K), lambda i, j: (i, 0)),
                pl.BlockSpec((K, tn), lambda i, j: (0, j))]
    args = [a, w]
    blocks = _nbytes((tm, K), a.dtype) + _nbytes((K, tn), w.dtype) + _nbytes((tm, tn), out_dtype)
    if bias is not None:
        in_specs.append(pl.BlockSpec((1, tn), lambda i, j: (0, j)))
        args.append(bias.reshape(1, N))
    if residual is not None:
        in_specs.append(pl.BlockSpec((tm, tn), lambda i, j: (i, j)))
        args.append(residual)
        blocks += _nbytes((tm, tn), residual.dtype)
    return pl.pallas_call(
        functools.partial(_matmul_kernel, scale=scale, has_bias=bias is not None,
                          has_res=residual is not None),
        grid=(T // tm, N // tn),
        in_specs=in_specs,
        out_specs=pl.BlockSpec((tm, tn), lambda i, j: (i, j)),
        out_shape=jax.ShapeDtypeStruct((T, N), out_dtype),
        compiler_params=pltpu.CompilerParams(
            dimension_semantics=("parallel", "parallel"),
            vmem_limit_bytes=_vmem_limit(blocks, 3 * _nbytes((tm, tn), F32))),
        name=name,
    )(*args)


def _dwconv_kernel(cur_ref, halo_ref, w_ref, b_ref, g_ref, beta_ref, o_ref, buf_ref, y_ref,
                   *, ts, d_model):
    i = pl.program_id(1)
    halo = halo_ref[0]
    buf_ref[0:CONV_HALO, :] = jnp.where(i > 0, halo, jnp.zeros_like(halo))
    buf_ref[CONV_HALO:CONV_HALO + ts, :] = cur_ref[0]
    first = CONV_HALO - (CONV_WIDTH - 1)

    def col_chunk(c, carry):
        cols = pl.ds(pl.multiple_of(c * LANES, LANES), LANES)
        acc = jnp.zeros((ts, LANES), F32) + b_ref[:, cols]
        for w in range(CONV_WIDTH):
            acc = acc + buf_ref[first + w:first + w + ts, cols] * w_ref[w:w + 1, cols]
        y_ref[:, cols] = acc
        return carry

    lax.fori_loop(0, d_model // LANES, col_chunk, 0)
    y = y_ref[...]
    mu = jnp.mean(y, axis=-1, keepdims=True)
    yc = y - mu
    var = jnp.mean(yc * yc, axis=-1, keepdims=True)
    z = yc * lax.rsqrt(var + NORM_EPS) * g_ref[...] + beta_ref[...]
    o_ref[0] = (z * jax.nn.sigmoid(z)).astype(o_ref.dtype)


def dwconv_ln_silu(u, dw_w, dw_b, ln_g, ln_b, ts=256):
    B, S, D = u.shape
    w_pad = jnp.zeros((CONV_HALO, D), F32).at[:CONV_WIDTH].set(dw_w)
    halo_per_tile = ts // CONV_HALO
    blocks = (_nbytes((ts, D), F32) + _nbytes((CONV_HALO, D), F32) * 2
              + _nbytes((ts, D), BF16) + _nbytes((2 * ts + CONV_HALO, D), F32))
    vec = lambda b, i: (0, 0)
    return pl.pallas_call(
        functools.partial(_dwconv_kernel, ts=ts, d_model=D),
        grid=(B, S // ts),
        in_specs=[pl.BlockSpec((1, ts, D), lambda b, i: (b, i, 0)),
                  pl.BlockSpec((1, CONV_HALO, D),
                               lambda b, i: (b, jnp.maximum(i * halo_per_tile - 1, 0), 0)),
                  pl.BlockSpec((CONV_HALO, D), vec),
                  pl.BlockSpec((1, D), vec),
                  pl.BlockSpec((1, D), vec),
                  pl.BlockSpec((1, D), vec)],
        out_specs=pl.BlockSpec((1, ts, D), lambda b, i: (b, i, 0)),
        out_shape=jax.ShapeDtypeStruct((B, S, D), BF16),
        scratch_shapes=[pltpu.VMEM((CONV_HALO + ts, D), F32),
                        pltpu.VMEM((ts, D), F32)],
        compiler_params=pltpu.CompilerParams(
            dimension_semantics=("parallel", "arbitrary"),
            vmem_limit_bytes=_vmem_limit(blocks, 4 * _nbytes((ts, D), F32))),
        name="dwconv_ln_silu",
    )(u, u, w_pad, dw_b.reshape(1, D), ln_g.reshape(1, D), ln_b.reshape(1, D))


def _forget_cumsum_kernel(f_ref, b_ref, o_ref, carry_ref, *, ts):
    @pl.when(pl.program_id(1) == 0)
    def _():
        carry_ref[...] = jnp.zeros_like(carry_ref)

    x = f_ref[0] + b_ref[...]
    x = jnp.minimum(x, 0.0) - jnp.log1p(jnp.exp(-jnp.abs(x)))
    row = lax.broadcasted_iota(jnp.int32, x.shape, 0)
    k = 1
    while k < ts:
        shifted = pltpu.roll(x, k, axis=0)
        x = x + jnp.where(row >= k, shifted, 0.0)
        k *= 2
    x = x + carry_ref[...]
    o_ref[0] = x
    carry_ref[...] = x[ts - 1:ts, :]


def forget_cumsum(f, b_pad, ts=512):
    B, S, L = f.shape
    return pl.pallas_call(
        functools.partial(_forget_cumsum_kernel, ts=ts),
        grid=(B, S // ts),
        in_specs=[pl.BlockSpec((1, ts, L), lambda b, i: (b, i, 0)),
                  pl.BlockSpec((1, L), lambda b, i: (0, 0))],
        out_specs=pl.BlockSpec((1, ts, L), lambda b, i: (b, i, 0)),
        out_shape=jax.ShapeDtypeStruct((B, S, L), F32),
        scratch_shapes=[pltpu.VMEM((1, L), F32)],
        compiler_params=pltpu.CompilerParams(
            dimension_semantics=("parallel", "arbitrary")),
        name="forget_cumsum",
    )(f, b_pad)


def _attn_kernel(q_ref, k_ref, v_ref, ck_ref, cq_ref, o_ref, *, tq, tk):
    i = pl.program_id(2)
    q = q_ref[...]
    cq = cq_ref[0, 0]
    blocks_per_q = tq // tk

    def step(j, carry, masked_block):
        m, l, acc = carry
        rows = pl.ds(pl.multiple_of(j * tk, tk), tk)
        kj = k_ref[rows, :]
        vj = v_ref[rows, :]
        ckj = ck_ref[0, 0, pl.ds(j, 1), :]
        z = lax.dot_general(q, kj, (((1,), (1,)), ((), ())), preferred_element_type=F32)
        z = z - ckj
        if masked_block is not None:
            r = lax.broadcasted_iota(jnp.int32, (tq, tk), 0)
            c = lax.broadcasted_iota(jnp.int32, (tq, tk), 1) + masked_block * tk
            z = jnp.where(c <= r, z, -jnp.inf)
        m_new = jnp.maximum(m, jnp.max(z, axis=1, keepdims=True) + cq)
        p = jnp.exp(z + (cq - m_new))
        alpha = jnp.exp(m - m_new)
        l = alpha * l + jnp.sum(p, axis=1, keepdims=True)
        acc = alpha * acc + jnp.dot(p.astype(vj.dtype), vj, preferred_element_type=F32)
        return m_new, l, acc

    init = (jnp.full((tq, 1), -jnp.inf, F32), jnp.zeros((tq, 1), F32),
            jnp.zeros((tq, HEAD_DIM), F32))
    n_full = i * blocks_per_q
    carry = lax.fori_loop(0, n_full, functools.partial(step, masked_block=None), init)
    for d in range(blocks_per_q):
        carry = step(n_full + d, carry, masked_block=d)
    _, l, acc = carry
    o_ref[...] = (acc / l).astype(o_ref.dtype)


def fox_attention(q, kv, ck, cq, batch, seq, n_heads, tq=512, tk=512):
    T, D = q.shape
    nq = seq // tq
    blocks = (2 * _nbytes((seq, HEAD_DIM), BF16) + 2 * _nbytes((tq, HEAD_DIM), BF16)
              + _nbytes((seq,), F32) + _nbytes((tq, LANES), F32))
    return pl.pallas_call(
        functools.partial(_attn_kernel, tq=tq, tk=tk),
        grid=(batch, n_heads, nq),
        in_specs=[pl.BlockSpec((tq, HEAD_DIM), lambda b, h, i: (b * nq + i, h)),
                  pl.BlockSpec((seq, HEAD_DIM), lambda b, h, i: (b, h)),
                  pl.BlockSpec((seq, HEAD_DIM), lambda b, h, i: (b, n_heads + h)),
                  pl.BlockSpec((1, 1, seq // tk, tk), lambda b, h, i: (b, h, 0, 0)),
                  pl.BlockSpec((1, 1, tq, 1), lambda b, h, i: (b, h, i, 0))],
        out_specs=pl.BlockSpec((tq, HEAD_DIM), lambda b, h, i: (b * nq + i, h)),
        out_shape=jax.ShapeDtypeStruct((T, D), BF16),
        compiler_params=pltpu.CompilerParams(
            dimension_semantics=("parallel", "parallel", "arbitrary"),
            vmem_limit_bytes=_vmem_limit(blocks, 6 * _nbytes((tq, tk), F32))),
        name="fox_attention",
    )(q, kv, kv, ck, cq)


def _ffn_half_step(h, g, w1, w3, w2):
    hn = rmsnorm(h, g, BF16)
    d_ff = w1.shape[1]
    gate = gated_matmul(hn, w1, d_ff, 0, None, "swiglu", BF16, tm=1024, tn=256, w2=w3)
    return matmul(gate, w2, F32, tm=512, tn=256, residual=h, scale=0.5, name="ffn_down")


def kernel(x, norm_ffn1, ffn1_w1, ffn1_w3, ffn1_w2, norm_mix, norm_ffn2, ffn2_w1, ffn2_w3,
           ffn2_w2, conv_pw1_w, conv_pw1_b, conv_dw_w, conv_dw_b, conv_ln_g, conv_ln_b,
           conv_pw2_w, conv_pw2_b, kv_norm, w_kvf, b_f, attn_wq, attn_wo, final_norm):
    B, S, D = x.shape
    T = B * S
    depth = norm_ffn1.shape[0]
    n_a = conv_pw1_w.shape[0]
    n_heads = D // HEAD_DIM
    tk = 512
    bf = lambda w: w.astype(BF16)

    h = x.reshape(T, D)
    kv = ck = cq = None
    for layer in range(depth):
        h = _ffn_half_step(h, norm_ffn1[layer], bf(ffn1_w1[layer]), bf(ffn1_w3[layer]),
                           bf(ffn1_w2[layer]))
        hn = rmsnorm(h, norm_mix[layer], BF16)
        if layer < n_a:
            i = layer
            u = gated_matmul(hn, bf(conv_pw1_w[i]), D, D, conv_pw1_b[i], "glu", F32,
                             tm=1024, tn=512)
            act = dwconv_ln_silu(u.reshape(B, S, D), conv_dw_w[i], conv_dw_b[i],
                                 conv_ln_g[i], conv_ln_b[i]).reshape(T, D)
            h = matmul(act, bf(conv_pw2_w[i]), F32, tm=1024, tn=512, bias=conv_pw2_b[i],
                       residual=h, name="conv_pw2")
        else:
            j = layer - n_a
            q = matmul(hn, bf(attn_wq[j]), BF16, tm=1024, tn=512,
                       scale=1.0 / math.sqrt(HEAD_DIM), name="q_proj")
            o = fox_attention(q, kv, ck, cq, B, S, n_heads, tk=tk)
            h = matmul(o, bf(attn_wo[j]), F32, tm=1024, tn=512, residual=h, name="o_proj")
        h = _ffn_half_step(h, norm_ffn2[layer], bf(ffn2_w1[layer]), bf(ffn2_w3[layer]),
                           bf(ffn2_w2[layer]))
        if layer == n_a - 1:
            yn = rmsnorm(h, kv_norm, BF16)
            kv = matmul(yn, bf(w_kvf[:, :2 * D]), BF16, tm=1024, tn=512, name="kv_proj")
            w_f = jnp.zeros((D, LANES), BF16).at[:, :n_heads].set(bf(w_kvf[:, 2 * D:]))
            f = matmul(yn, w_f, F32, tm=1024, tn=LANES, name="forget_proj")
            b_pad = jnp.zeros((1, LANES), F32).at[0, :n_heads].set(b_f)
            c = forget_cumsum(f.reshape(B, S, LANES), b_pad)
            c = c[:, :, :n_heads].transpose(0, 2, 1)
            ck = c.reshape(B, n_heads, S // tk, tk)
            cq = c.reshape(B, n_heads, S, 1)
    return rmsnorm(h, final_norm, F32).reshape(B, S, D)
```

```python
import functools
import math

import jax
import jax.numpy as jnp
from jax import lax
from jax.experimental import pallas as pl
from jax.experimental.pallas import tpu as pltpu

NORM_EPS = 1e-6
LOG2_E = math.log2(math.e)
HEAD_DIM = 128
CONV_WIDTH = 31
CONV_HALO = 32
LANES = 128
V7X_VMEM_BYTES = 64 * 1024 * 1024
VMEM_CEILING = V7X_VMEM_BYTES - 8 * 1024 * 1024

F32 = jnp.float32
BF16 = jnp.bfloat16


def _vmem_limit(block_bytes, temp_bytes=0):
    need = 2 * block_bytes + temp_bytes + 4 * 1024 * 1024
    return int(min(max(need, 16 * 1024 * 1024), VMEM_CEILING))


def _nbytes(shape, dtype):
    return math.prod(shape) * jnp.dtype(dtype).itemsize


def _rmsnorm_kernel(x_ref, g_ref, o_ref):
    x = x_ref[...]
    ms = jnp.mean(x * x, axis=-1, keepdims=True)
    o_ref[...] = (x * lax.rsqrt(ms + NORM_EPS) * g_ref[...]).astype(o_ref.dtype)


def rmsnorm(x, g, out_dtype, tm=256):
    T, D = x.shape
    blocks = _nbytes((tm, D), F32) + _nbytes((tm, D), out_dtype)
    return pl.pallas_call(
        _rmsnorm_kernel,
        grid=(T // tm,),
        in_specs=[pl.BlockSpec((tm, D), lambda i: (i, 0)),
                  pl.BlockSpec((1, D), lambda i: (0, 0))],
        out_specs=pl.BlockSpec((tm, D), lambda i: (i, 0)),
        out_shape=jax.ShapeDtypeStruct((T, D), out_dtype),
        compiler_params=pltpu.CompilerParams(
            dimension_semantics=("parallel",),
            vmem_limit_bytes=_vmem_limit(blocks, 2 * _nbytes((tm, D), F32))),
        name="rmsnorm",
    )(x, g.reshape(1, D))


def _gated_kernel(*refs, mode, has_bias):
    if has_bias:
        a_ref, wu_ref, wv_ref, bu_ref, bv_ref, o_ref = refs
    else:
        a_ref, wu_ref, wv_ref, o_ref = refs
    a = a_ref[...]
    u = jnp.dot(a, wu_ref[...].astype(BF16), preferred_element_type=F32)
    v = jnp.dot(a, wv_ref[...].astype(BF16), preferred_element_type=F32)
    if has_bias:
        u = u + bu_ref[...]
        v = v + bv_ref[...]
    if mode == "swiglu":
        r = u * jax.nn.sigmoid(u) * v
    else:
        r = u * jax.nn.sigmoid(v)
    o_ref[...] = r.astype(o_ref.dtype)


def gated_matmul(a, wu, wv, layer, n_out, v_col_offset, bias, mode, out_dtype, tm, tn):
    T, K = a.shape
    off = v_col_offset // tn
    has_bias = bias is not None
    in_specs = [pl.BlockSpec((tm, K), lambda i, j: (i, 0)),
                pl.BlockSpec((None, K, tn), lambda i, j: (layer, 0, j)),
                pl.BlockSpec((None, K, tn), lambda i, j: (layer, 0, j + off))]
    args = [a, wu, wv]
    if has_bias:
        b2 = bias.reshape(1, -1)
        in_specs += [pl.BlockSpec((1, tn), lambda i, j: (0, j)),
                     pl.BlockSpec((1, tn), lambda i, j: (0, j + off))]
        args += [b2, b2]
    blocks = (_nbytes((tm, K), a.dtype) + 2 * _nbytes((K, tn), wu.dtype)
              + _nbytes((tm, tn), out_dtype))
    temps = 5 * _nbytes((tm, tn), F32) + 2 * _nbytes((K, tn), BF16)
    return pl.pallas_call(
        functools.partial(_gated_kernel, mode=mode, has_bias=has_bias),
        grid=(T // tm, n_out // tn),
        in_specs=in_specs,
        out_specs=pl.BlockSpec((tm, tn), lambda i, j: (i, j)),
        out_shape=jax.ShapeDtypeStruct((T, n_out), out_dtype),
        compiler_params=pltpu.CompilerParams(
            dimension_semantics=("parallel", "parallel"),
            vmem_limit_bytes=_vmem_limit(blocks, temps)),
        name="gated_" + mode,
    )(*args)


def _matmul_kernel(*refs, scale, has_bias, has_res):
    a_ref, w_ref = refs[0], refs[1]
    o_ref = refs[-1]
    acc = jnp.dot(a_ref[...], w_ref[...].astype(BF16), preferred_element_type=F32)
    pos = 2
    if has_bias:
        acc = acc + refs[pos][...]
        pos += 1
    if scale != 1.0:
        acc = acc * scale
    if has_res:
        acc = acc + refs[pos][...]
    o_ref[...] = acc.astype(o_ref.dtype)


def matmul(a, w, layer, out_dtype, tm, tn, n_out=None, bias=None, residual=None, scale=1.0,
           name="matmul"):
    T, K = a.shape
    N = w.shape[2] if n_out is None else n_out
    in_specs = [pl.BlockSpec((tm, K), lambda i, j: (i, 0)),
                pl.BlockSpec((None, K, tn), lambda i, j: (layer, 0, j))]
    args = [a, w]
    blocks = _nbytes((tm, K), a.dtype) + _nbytes((K, tn), w.dtype) + _nbytes((tm, tn), out_dtype)
    if bias is not None:
        in_specs.append(pl.BlockSpec((1, tn), lambda i, j: (0, j)))
        args.append(bias.reshape(1, N))
    if residual is not None:
        in_specs.append(pl.BlockSpec((tm, tn), lambda i, j: (i, j)))
        args.append(residual)
        blocks += _nbytes((tm, tn), residual.dtype)
    return pl.pallas_call(
        functools.partial(_matmul_kernel, scale=scale, has_bias=bias is not None,
                          has_res=residual is not None),
        grid=(T // tm, N // tn),
        in_specs=in_specs,
        out_specs=pl.BlockSpec((tm, tn), lambda i, j: (i, j)),
        out_shape=jax.ShapeDtypeStruct((T, N), out_dtype),
        compiler_params=pltpu.CompilerParams(
            dimension_semantics=("parallel", "parallel"),
            vmem_limit_bytes=_vmem_limit(
                blocks, 3 * _nbytes((tm, tn), F32) + _nbytes((K, tn), BF16))),
        name=name,
    )(*args)


def _dwconv_kernel(cur_ref, halo_ref, w_ref, b_ref, g_ref, beta_ref, o_ref, buf_ref, y_ref,
                   *, ts, d_model):
    i = pl.program_id(1)
    halo = halo_ref[0]
    buf_ref[0:CONV_HALO, :] = jnp.where(i > 0, halo, jnp.zeros_like(halo))
    buf_ref[CONV_HALO:CONV_HALO + ts, :] = cur_ref[0]
    first = CONV_HALO - (CONV_WIDTH - 1)

    def col_chunk(c, carry):
        cols = pl.ds(pl.multiple_of(c * LANES, LANES), LANES)
        acc = jnp.zeros((ts, LANES), F32) + b_ref[:, cols]
        for w in range(CONV_WIDTH):
            acc = acc + buf_ref[first + w:first + w + ts, cols] * w_ref[w:w + 1, cols]
        y_ref[:, cols] = acc
        return carry

    lax.fori_loop(0, d_model // LANES, col_chunk, 0)
    y = y_ref[...]
    mu = jnp.mean(y, axis=-1, keepdims=True)
    yc = y - mu
    var = jnp.mean(yc * yc, axis=-1, keepdims=True)
    z = yc * lax.rsqrt(var + NORM_EPS) * g_ref[...] + beta_ref[...]
    o_ref[0] = (z * jax.nn.sigmoid(z)).astype(o_ref.dtype)


def dwconv_ln_silu(u, dw_w, dw_b, ln_g, ln_b, ts=256):
    B, S, D = u.shape
    w_pad = jnp.zeros((CONV_HALO, D), F32).at[:CONV_WIDTH].set(dw_w)
    halo_per_tile = ts // CONV_HALO
    blocks = (_nbytes((ts, D), F32) + _nbytes((CONV_HALO, D), F32) * 2
              + _nbytes((ts, D), BF16) + _nbytes((2 * ts + CONV_HALO, D), F32))
    vec = lambda b, i: (0, 0)
    return pl.pallas_call(
        functools.partial(_dwconv_kernel, ts=ts, d_model=D),
        grid=(B, S // ts),
        in_specs=[pl.BlockSpec((1, ts, D), lambda b, i: (b, i, 0)),
                  pl.BlockSpec((1, CONV_HALO, D),
                               lambda b, i: (b, jnp.maximum(i * halo_per_tile - 1, 0), 0)),
                  pl.BlockSpec((CONV_HALO, D), vec),
                  pl.BlockSpec((1, D), vec),
                  pl.BlockSpec((1, D), vec),
                  pl.BlockSpec((1, D), vec)],
        out_specs=pl.BlockSpec((1, ts, D), lambda b, i: (b, i, 0)),
        out_shape=jax.ShapeDtypeStruct((B, S, D), BF16),
        scratch_shapes=[pltpu.VMEM((CONV_HALO + ts, D), F32),
                        pltpu.VMEM((ts, D), F32)],
        compiler_params=pltpu.CompilerParams(
            dimension_semantics=("parallel", "arbitrary"),
            vmem_limit_bytes=_vmem_limit(blocks, 4 * _nbytes((ts, D), F32))),
        name="dwconv_ln_silu",
    )(u, u, w_pad, dw_b.reshape(1, D), ln_g.reshape(1, D), ln_b.reshape(1, D))


def _forget_cumsum_kernel(f_ref, b_ref, o_ref, carry_ref, *, ts):
    @pl.when(pl.program_id(1) == 0)
    def _():
        carry_ref[...] = jnp.zeros_like(carry_ref)

    x = f_ref[0] + b_ref[...]
    x = jnp.minimum(x, 0.0) - jnp.log1p(jnp.exp(-jnp.abs(x)))
    row = lax.broadcasted_iota(jnp.int32, x.shape, 0)
    k = 1
    while k < ts:
        shifted = pltpu.roll(x, k, axis=0)
        x = x + jnp.where(row >= k, shifted, 0.0)
        k *= 2
    x = x + carry_ref[...]
    o_ref[0] = x * LOG2_E
    carry_ref[...] = x[ts - 1:ts, :]


def forget_cumsum(f, b_pad, ts=512):
    B, S, L = f.shape
    return pl.pallas_call(
        functools.partial(_forget_cumsum_kernel, ts=ts),
        grid=(B, S // ts),
        in_specs=[pl.BlockSpec((1, ts, L), lambda b, i: (b, i, 0)),
                  pl.BlockSpec((1, L), lambda b, i: (0, 0))],
        out_specs=pl.BlockSpec((1, ts, L), lambda b, i: (b, i, 0)),
        out_shape=jax.ShapeDtypeStruct((B, S, L), F32),
        scratch_shapes=[pltpu.VMEM((1, L), F32)],
        compiler_params=pltpu.CompilerParams(
            dimension_semantics=("parallel", "arbitrary")),
        name="forget_cumsum",
    )(f, b_pad)


def _lane_allmax(x):
    shift = LANES // 2
    while shift >= 1:
        x = jnp.maximum(x, pltpu.roll(x, shift, axis=1))
        shift //= 2
    return x


def _attn_kernel(q_ref, k_ref, v_ref, ck_ref, cq_ref, o_ref,
                 z_scr, p_scr, acc_scr, m_scr, alpha_scr, cqr_scr, *, tq, heads, chunk):
    i = pl.program_id(2)
    tk = tq
    lane = lax.broadcasted_iota(jnp.int32, (tk, HEAD_DIM), 1)
    ones_col = jnp.where(lane == 0, 1.0, 0.0).astype(BF16)
    n_lane_tiles = tk // LANES

    def head_cols(g):
        return slice(g * HEAD_DIM, (g + 1) * HEAD_DIM)

    def key_rows(j):
        return pl.ds(pl.multiple_of(j * tk, tk), tk)

    def scores(g, j):
        z = lax.dot_general(q_ref[:, head_cols(g)], k_ref[key_rows(j), head_cols(g)],
                            (((1,), (1,)), ((), ())), preferred_element_type=F32)
        z_scr[g] = z - ck_ref[0, g, pl.ds(j, 1), :]

    def softmax(g, diagonal):
        for c in range(tq // chunk):
            rows = slice(c * chunk, (c + 1) * chunk)
            z = [z_scr[g, rows, t * LANES:(t + 1) * LANES] for t in range(n_lane_tiles)]
            if diagonal:
                r = lax.broadcasted_iota(jnp.int32, (chunk, LANES), 0) + c * chunk
                col = lax.broadcasted_iota(jnp.int32, (chunk, LANES), 1)
                z = [jnp.where(col + t * LANES <= r, z[t], -jnp.inf)
                     for t in range(n_lane_tiles)]
            part = z[0]
            for t in range(1, n_lane_tiles):
                part = jnp.maximum(part, z[t])
            m_old = m_scr[g, rows, :]
            cqr = cqr_scr[g, rows, :]
            row_max = jnp.broadcast_to(jnp.max(part, axis=1, keepdims=True), (chunk, LANES))
            m_new = jnp.maximum(m_old, row_max + cqr)
            shift = cqr - m_new
            for t in range(n_lane_tiles):
                p_scr[g, rows, t * LANES:(t + 1) * LANES] = jnp.exp2(z[t] + shift).astype(BF16)
            alpha_scr[g, rows, :] = jnp.exp2(m_old - m_new)
            m_scr[g, rows, :] = m_new

    def values(g, j):
        vj = jnp.concatenate([v_ref[key_rows(j), head_cols(g)], ones_col], axis=1)
        pv = jnp.dot(p_scr[g], vj, preferred_element_type=F32)
        a = alpha_scr[g]
        acc_scr[g, :, :HEAD_DIM] = a * acc_scr[g, :, :HEAD_DIM] + pv[:, :HEAD_DIM]
        acc_scr[g, :, HEAD_DIM:] = a * acc_scr[g, :, HEAD_DIM:] + pv[:, HEAD_DIM:]

    def finish(g):
        num = acc_scr[g, :, :HEAD_DIM]
        den = acc_scr[g, :, HEAD_DIM:HEAD_DIM + 1]
        o_ref[:, head_cols(g)] = (num / den).astype(o_ref.dtype)

    for g in range(heads):
        m_scr[g] = jnp.full((tq, LANES), -jnp.inf, F32)
        acc_scr[g] = jnp.zeros((tq, 2 * HEAD_DIM), F32)
        cqr_scr[g] = jnp.broadcast_to(cq_ref[0, g], (tq, LANES))

    @pl.when(i == 0)
    def _():
        for g in range(heads):
            scores(g, 0)
            softmax(g, diagonal=True)
            values(g, 0)
            finish(g)

    @pl.when(i > 0)
    def _():
        for g in range(heads):
            scores(g, 0)
        for g in range(heads):
            softmax(g, diagonal=False)
        for g in range(heads):
            scores(g, 1)

        def trip(t, carry):
            for g in range(heads):
                values(g, t - 1)
            for g in range(heads):
                softmax(g, diagonal=False)
            for g in range(heads):
                scores(g, t + 1)
            return carry

        lax.fori_loop(1, i, trip, 0)
        for g in range(heads):
            values(g, i - 1)
        for g in range(heads):
            softmax(g, diagonal=True)
        for g in range(heads):
            values(g, i)
            finish(g)


def fox_attention(q, kv, ck, cq, batch, seq, n_heads, tq=512, heads=4, chunk=64):
    T, D = q.shape
    nq = seq // tq
    width = heads * HEAD_DIM
    n_groups = n_heads // heads
    blocks = (2 * _nbytes((seq, width), BF16) + 2 * _nbytes((tq, width), BF16)
              + heads * _nbytes((seq,), F32) + heads * _nbytes((tq, LANES), F32))
    scratch = [pltpu.VMEM((heads, tq, tq), F32),
               pltpu.VMEM((heads, tq, tq), BF16),
               pltpu.VMEM((heads, tq, 2 * HEAD_DIM), F32),
               pltpu.VMEM((heads, tq, LANES), F32),
               pltpu.VMEM((heads, tq, LANES), F32),
               pltpu.VMEM((heads, tq, LANES), F32)]
    scratch_bytes = heads * (_nbytes((tq, tq), F32) + _nbytes((tq, tq), BF16)
                             + _nbytes((tq, 5 * LANES), F32))
    return pl.pallas_call(
        functools.partial(_attn_kernel, tq=tq, heads=heads, chunk=chunk),
        grid=(batch, n_groups, nq),
        in_specs=[pl.BlockSpec((tq, width), lambda b, h, i: (b * nq + i, h)),
                  pl.BlockSpec((seq, width), lambda b, h, i: (b, h)),
                  pl.BlockSpec((seq, width), lambda b, h, i: (b, n_groups + h)),
                  pl.BlockSpec((1, heads, seq // tq, tq), lambda b, h, i: (b, h, 0, 0)),
                  pl.BlockSpec((1, heads, tq, 1), lambda b, h, i: (b, h, i, 0))],
        out_specs=pl.BlockSpec((tq, width), lambda b, h, i: (b * nq + i, h)),
        out_shape=jax.ShapeDtypeStruct((T, D), BF16),
        scratch_shapes=scratch,
        compiler_params=pltpu.CompilerParams(
            dimension_semantics=("parallel", "parallel", "arbitrary"),
            vmem_limit_bytes=_vmem_limit(blocks, scratch_bytes + 4 * _nbytes((tq, tq), F32))),
        name="fox_attention",
    )(q, kv, kv, ck, cq)


def _ffn_half_step(h, g, w1, w3, w2, layer):
    hn = rmsnorm(h, g[layer], BF16)
    d_ff = w1.shape[2]
    gate = gated_matmul(hn, w1, w3, layer, d_ff, 0, None, "swiglu", BF16, tm=1024, tn=256)
    w2_bf = w2[layer].astype(BF16)[None]
    return matmul(gate, w2_bf, 0, F32, tm=512, tn=512, residual=h, scale=0.5, name="ffn_down")


def kernel(x, norm_ffn1, ffn1_w1, ffn1_w3, ffn1_w2, norm_mix, norm_ffn2, ffn2_w1, ffn2_w3,
           ffn2_w2, conv_pw1_w, conv_pw1_b, conv_dw_w, conv_dw_b, conv_ln_g, conv_ln_b,
           conv_pw2_w, conv_pw2_b, kv_norm, w_kvf, b_f, attn_wq, attn_wo, final_norm):
    B, S, D = x.shape
    T = B * S
    depth = norm_ffn1.shape[0]
    n_a = conv_pw1_w.shape[0]
    n_heads = D // HEAD_DIM
    tq = 512

    h = x.reshape(T, D)
    kv = ck = cq = None
    for layer in range(depth):
        h = _ffn_half_step(h, norm_ffn1, ffn1_w1, ffn1_w3, ffn1_w2, layer)
        hn = rmsnorm(h, norm_mix[layer], BF16)
        if layer < n_a:
            i = layer
            u = gated_matmul(hn, conv_pw1_w, conv_pw1_w, i, D, D, conv_pw1_b[i], "glu", F32,
                             tm=1024, tn=256)
            act = dwconv_ln_silu(u.reshape(B, S, D), conv_dw_w[i], conv_dw_b[i],
                                 conv_ln_g[i], conv_ln_b[i]).reshape(T, D)
            h = matmul(act, conv_pw2_w, i, F32, tm=1024, tn=512, bias=conv_pw2_b[i],
                       residual=h, name="conv_pw2")
        else:
            j = layer - n_a
            q = matmul(hn, attn_wq, j, BF16, tm=1024, tn=512,
                       scale=LOG2_E / math.sqrt(HEAD_DIM), name="q_proj")
            o = fox_attention(q, kv, ck, cq, B, S, n_heads, tq=tq)
            h = matmul(o, attn_wo, j, F32, tm=1024, tn=512, residual=h, name="o_proj")
        h = _ffn_half_step(h, norm_ffn2, ffn2_w1, ffn2_w3, ffn2_w2, layer)
        if layer == n_a - 1:
            yn = rmsnorm(h, kv_norm, BF16)
            kv = matmul(yn, w_kvf[None], 0, BF16, tm=1024, tn=512, n_out=2 * D, name="kv_proj")
            w_f = jnp.zeros((1, D, LANES), F32).at[0, :, :n_heads].set(w_kvf[:, 2 * D:])
            f = matmul(yn, w_f, 0, F32, tm=1024, tn=LANES, name="forget_proj")
            b_pad = jnp.zeros((1, LANES), F32).at[0, :n_heads].set(b_f)
            c = forget_cumsum(f.reshape(B, S, LANES), b_pad)
            c = c[:, :, :n_heads].transpose(0, 2, 1)
            ck = c.reshape(B, n_heads, S // tq, tq)
            cq = c.reshape(B, n_heads, S, 1)
    return rmsnorm(h, final_norm, F32).reshape(B, S, D)
```
